```python
import jax, jax.numpy as jnp
from jax import lax
import numpy as np

D_MODEL = 2048
BATCH = 1
SEQ = 8192
DEPTH = 1

A_HEADS = 4
A_DK = 256
A_DV = 512
B_HEADS = 16
B_DK = 128
B_DV = 128
CONV_K = 5
CHUNK = 64
N_BRANCH = 2
D_FF = -(-8 * D_MODEL // (3 * 256)) * 256
EPS = 1e-6
ADA_SCALE = 0.5

A_QK = A_HEADS * A_DK
A_V = A_HEADS * A_DV
B_QKV = 2 * B_HEADS * B_DK + B_HEADS * B_DV
B_V = B_HEADS * B_DV
IN_SIZES = (
    A_QK,
    A_QK,
    A_V,
    A_V,
    2 * A_HEADS,
    2 * A_HEADS,
    B_QKV,
    B_V,
    2 * B_HEADS,
    2 * B_HEADS,
    N_BRANCH * D_MODEL,
)
IN_SPLITS = tuple(int(s) for s in np.cumsum(IN_SIZES)[:-1])
D_IN = int(sum(IN_SIZES))

kernel_name = 'hybrid_mlstm_gdn_bidir_block'


def rms_norm(x, w):
    xf = x.astype(jnp.float32)
    y = xf * lax.rsqrt(jnp.mean(xf * xf, axis=-1, keepdims=True) + EPS)
    return (y * w.astype(jnp.float32)).astype(x.dtype)


def l2_normalize(a):
    return a * lax.rsqrt(jnp.sum(a * a, axis=-1, keepdims=True) + EPS)


def to_chunks(a):
    b, h, t = a.shape[:3]
    a = a.reshape((b, h, t // CHUNK, CHUNK) + a.shape[3:])
    return jnp.moveaxis(a, 2, 0)


def from_chunks(a):
    a = jnp.moveaxis(a, 0, 2)
    b, h, n, l = a.shape[:4]
    return a.reshape((b, h, n * l) + a.shape[4:])


def mlstm_chunkwise(q, k, v, log_i, log_f):
    b, h, _, dk = q.shape
    dv = v.shape[-1]
    qc, kc, vc, lic = (to_chunks(a) for a in (q, k, v, log_i))
    fc = jnp.cumsum(to_chunks(log_f), axis=-1)
    lower = jnp.tril(jnp.ones((CHUNK, CHUNK), dtype=bool))

    def step(carry, inp):
        c_mem, n_mem, m_mem = carry
        q_, k_, v_, li, fcum = inp
        d_log = jnp.where(lower, fcum[..., :, None] - fcum[..., None, :] + li[..., None, :], -jnp.inf)
        inter_log = fcum + m_mem[..., None]
        m_t = jnp.maximum(inter_log, jnp.max(d_log, axis=-1))
        p = jnp.exp(d_log - m_t[..., None]) * jnp.einsum('bhtd,bhsd->bhts', q_, k_)
        w_inter = jnp.exp(inter_log - m_t)
        num = (w_inter[..., None] * jnp.einsum('bhtd,bhde->bhte', q_, c_mem)
               + jnp.einsum('bhts,bhse->bhte', p, v_))
        den = w_inter * jnp.einsum('bhtd,bhd->bht', q_, n_mem) + jnp.sum(p, axis=-1)
        h_t = num / jnp.maximum(jnp.abs(den), jnp.exp(-m_t))[..., None]
        f_tot = fcum[..., -1]
        a_log = f_tot[..., None] - fcum + li
        m_new = jnp.maximum(f_tot + m_mem, jnp.max(a_log, axis=-1))
        w_s = jnp.exp(a_log - m_new[..., None])
        carry_decay = jnp.exp(f_tot + m_mem - m_new)
        c_mem = carry_decay[..., None, None] * c_mem + jnp.einsum('bhs,bhsd,bhse->bhde', w_s, k_, v_)
        n_mem = carry_decay[..., None] * n_mem + jnp.einsum('bhs,bhsd->bhd', w_s, k_)
        return (c_mem, n_mem, m_new), h_t

    init = (jnp.zeros((b, h, dk, dv), jnp.float32),
            jnp.zeros((b, h, dk), jnp.float32),
            jnp.zeros((b, h), jnp.float32))
    _, hs = lax.scan(step, init, (qc, kc, vc, lic, fc))
    return from_chunks(hs)


def gated_delta_chunkwise(q, k, v, beta, g):
    b, h, _, dk = q.shape
    dv = v.shape[-1]
    qc, kc, vc, betac = (to_chunks(a) for a in (q, k, v, beta))
    gc = jnp.cumsum(to_chunks(g), axis=-1)
    lower = jnp.tril(jnp.ones((CHUNK, CHUNK), dtype=bool))
    strict = jnp.tril(jnp.ones((CHUNK, CHUNK), dtype=bool), -1)
    decay_lower = jnp.exp(jnp.where(lower, gc[..., :, None] - gc[..., None, :], -jnp.inf))
    decay_strict = jnp.where(strict, decay_lower, 0.0)
    a_mat = jnp.eye(CHUNK, dtype=jnp.float32) + (
        betac[..., :, None] * jnp.einsum('nbhid,nbhjd->nbhij', kc, kc) * decay_strict)
    t_mat = lax.linalg.triangular_solve(
        a_mat, jnp.broadcast_to(jnp.eye(CHUNK, dtype=jnp.float32), a_mat.shape),
        left_side=True, lower=True, unit_diagonal=True)
    u = jnp.einsum('nbhij,nbhje->nbhie', t_mat, betac[..., None] * vc)
    w = jnp.einsum('nbhij,nbhjd->nbhid', t_mat, (betac * jnp.exp(gc))[..., None] * kc)

    def step(s_mem, inp):
        q_, k_, u_, w_, g_, dmask = inp
        attn = jnp.einsum('bhid,bhjd->bhij', q_, k_) * dmask
        v_new = u_ - jnp.einsum('bhid,bhde->bhie', w_, s_mem)
        o = (jnp.einsum('bhid,bhde->bhie', q_ * jnp.exp(g_)[..., None], s_mem)
             + jnp.einsum('bhij,bhje->bhie', attn, v_new))
        g_last = g_[..., -1]
        s_mem = (jnp.exp(g_last)[..., None, None] * s_mem
                 + jnp.einsum('bhjd,bhje->bhde', k_ * jnp.exp(g_last[..., None] - g_)[..., None], v_new))
        return s_mem, o

    _, os_ = lax.scan(step, jnp.zeros((b, h, dk, dv), jnp.float32), (qc, kc, u, w, gc, decay_lower))
    return from_chunks(os_)


def bidirectional(fn, seq_inputs, gates_fwd, gates_bwd):
    rev = lambda a: jnp.flip(a, axis=2)
    out_f = fn(*seq_inputs, *gates_fwd)
    out_b = rev(fn(*(rev(a) for a in seq_inputs), *(rev(a) for a in gates_bwd)))
    return out_f + out_b


def centred_depthwise_conv(x, w):
    ch = x.shape[-1]
    return lax.conv_general_dilated(
        x, w[:, None, :].astype(x.dtype), window_strides=(1,),
        padding=[(CONV_K // 2, CONV_K // 2)],
        dimension_numbers=('NWC', 'WIO', 'NWC'), feature_group_count=ch)


def hybrid_layer(x, c_act, ada_w, ada_b, norm_mix_w, w_in, mlstm_b_i, mlstm_b_f, mlstm_norm_w,
                 gdn_conv_w, gdn_A_log, gdn_dt_bias, gdn_norm_w, w_branch_a, w_branch_b, w_out,
                 norm_ffn_w, w_gate_up, w_down):
    bsz, t, _ = x.shape
    f32 = jnp.float32
    mod = c_act @ ada_w + ada_b
    shift_m, scale_m, gate_m, shift_f, scale_f, gate_f = jnp.split(mod[:, None, :], 6, axis=-1)

    h = rms_norm(x, norm_mix_w) * (1 + scale_m) + shift_m
    proj = h @ w_in
    (a_q, a_k, a_v, a_o, a_i, a_f, b_qkv, b_z, b_beta, b_a, merge) = jnp.split(proj, IN_SPLITS, axis=-1)

    def heads(a, n, d):
        return a.reshape(bsz, t, n, d).transpose(0, 2, 1, 3).astype(f32)

    def dir_heads(a, n):
        return a.reshape(bsz, t, 2, n).astype(f32).transpose(2, 0, 3, 1)

    q_a = heads(a_q, A_HEADS, A_DK) * (A_DK ** -0.5)
    k_a = heads(a_k, A_HEADS, A_DK)
    v_a = heads(a_v, A_HEADS, A_DV)
    log_i = dir_heads(a_i, A_HEADS) + mlstm_b_i.astype(f32)[:, None, :, None]
    log_f = jax.nn.log_sigmoid(dir_heads(a_f, A_HEADS) + mlstm_b_f.astype(f32)[:, None, :, None])
    h_a = bidirectional(mlstm_chunkwise, (q_a, k_a, v_a), (log_i[0], log_f[0]), (log_i[1], log_f[1]))
    h_a = rms_norm(h_a.transpose(0, 2, 1, 3), mlstm_norm_w.reshape(A_HEADS, A_DV)).reshape(bsz, t, A_V)
    y_a = (jax.nn.sigmoid(a_o.astype(f32)) * h_a).astype(x.dtype) @ w_branch_a

    qkv = jax.nn.silu(centred_depthwise_conv(b_qkv, gdn_conv_w))
    b_q, b_k, b_v = jnp.split(qkv, [B_HEADS * B_DK, 2 * B_HEADS * B_DK], axis=-1)
    q_b = l2_normalize(heads(b_q, B_HEADS, B_DK)) * (B_DK ** -0.5)
    k_b = l2_normalize(heads(b_k, B_HEADS, B_DK))
    v_b = heads(b_v, B_HEADS, B_DV)
    beta = jax.nn.sigmoid(dir_heads(b_beta, B_HEADS))
    log_decay = -jnp.exp(gdn_A_log.astype(f32))[:, None, :, None] * jax.nn.softplus(
        dir_heads(b_a, B_HEADS) + gdn_dt_bias.astype(f32)[:, None, :, None])
    o_b = bidirectional(gated_delta_chunkwise, (q_b, k_b, v_b), (beta[0], log_decay[0]), (beta[1], log_decay[1]))
    o_b = rms_norm(o_b.transpose(0, 2, 1, 3), gdn_norm_w.reshape(B_HEADS, B_DV)).reshape(bsz, t, B_V)
    y_b = (o_b * jax.nn.silu(b_z.astype(f32))).astype(x.dtype) @ w_branch_b

    g_a, g_b = jnp.split(jax.nn.sigmoid(merge), N_BRANCH, axis=-1)
    x = x + gate_m * ((g_a * y_a + g_b * y_b) @ w_out)

    hf = rms_norm(x, norm_ffn_w) * (1 + scale_f) + shift_f
    gt, up = jnp.split(hf @ w_gate_up, 2, axis=-1)
    x = x + gate_f * ((jax.nn.silu(gt) * up) @ w_down)
    return x


def setup_inputs(seed: int = 0) -> dict:
    key = jax.random.key(seed)
    ks = jax.random.split(key, 24)
    f32 = jnp.float32
    nrm = lambda k, shape, s: jax.random.normal(k, shape, f32) * s
    gain = lambda k, shape: 1.0 + 0.02 * jax.random.normal(k, shape, f32)
    dt = jnp.exp(jax.random.uniform(ks[10], (DEPTH, 2, B_HEADS), f32, np.log(1e-3), np.log(1e-1)))
    return {
        'x': nrm(ks[0], (BATCH, SEQ, D_MODEL), 1.0),
        'c': nrm(ks[1], (BATCH, D_MODEL), 1.0),
        'ada_w': nrm(ks[2], (DEPTH, D_MODEL, 6 * D_MODEL), ADA_SCALE * D_MODEL ** -0.5),
        'ada_b': nrm(ks[3], (DEPTH, 6 * D_MODEL), 0.02),
        'norm_mix_w': gain(ks[4], (DEPTH, D_MODEL)),
        'w_in': nrm(ks[5], (DEPTH, D_MODEL, D_IN), D_MODEL ** -0.5),
        'mlstm_b_i': nrm(ks[6], (DEPTH, 2, A_HEADS), 0.1),
        'mlstm_b_f': jnp.linspace(3.0, 6.0, A_HEADS, dtype=f32) + nrm(ks[7], (DEPTH, 2, A_HEADS), 0.1),
        'mlstm_norm_w': gain(ks[8], (DEPTH, A_V)),
        'gdn_conv_w': nrm(ks[9], (DEPTH, CONV_K, B_QKV), CONV_K ** -0.5),
        'gdn_A_log': jnp.log(jax.random.uniform(ks[11], (DEPTH, 2, B_HEADS), f32, 1.0, 16.0)),
        'gdn_dt_bias': dt + jnp.log(-jnp.expm1(-dt)),
        'gdn_norm_w': gain(ks[12], (DEPTH, B_V)),
        'w_branch_a': nrm(ks[13], (DEPTH, A_V, D_MODEL), A_V ** -0.5),
        'w_branch_b': nrm(ks[14], (DEPTH, B_V, D_MODEL), B_V ** -0.5),
        'w_out': nrm(ks[15], (DEPTH, D_MODEL, D_MODEL), D_MODEL ** -0.5),
        'norm_ffn_w': gain(ks[16], (DEPTH, D_MODEL)),
        'w_gate_up': nrm(ks[17], (DEPTH, D_MODEL, 2 * D_FF), D_MODEL ** -0.5),
        'w_down': nrm(ks[18], (DEPTH, D_FF, D_MODEL), D_FF ** -0.5),
        'norm_final_w': gain(ks[19], (D_MODEL,)),
    }


def reference(x, c, ada_w, ada_b, norm_mix_w, w_in, mlstm_b_i, mlstm_b_f, mlstm_norm_w, gdn_conv_w,
              gdn_A_log, gdn_dt_bias, gdn_norm_w, w_branch_a, w_branch_b, w_out, norm_ffn_w,
              w_gate_up, w_down, norm_final_w):
    c_act = jax.nn.silu(c)
    for l in range(DEPTH):
        x = hybrid_layer(x, c_act, ada_w[l], ada_b[l], norm_mix_w[l], w_in[l], mlstm_b_i[l], mlstm_b_f[l],
                         mlstm_norm_w[l], gdn_conv_w[l], gdn_A_log[l], gdn_dt_bias[l], gdn_norm_w[l],
                         w_branch_a[l], w_branch_b[l], w_out[l], norm_ffn_w[l], w_gate_up[l], w_down[l])
    return rms_norm(x, norm_final_w)
```

```python
import functools

import jax
import jax.numpy as jnp
from jax import lax
from jax.experimental import pallas as pl
from jax.experimental.pallas import tpu as pltpu

F32 = jnp.float32
BF16 = jnp.bfloat16
EPS = 1e-6

LANES = 128
GATE_COLS = LANES
MLSTM_CHUNK = 256
GDN_CHUNK = 64
VMEM_LIMIT_BYTES = 56 * 1024 * 1024


def _cparams(*sem):
    return pltpu.CompilerParams(dimension_semantics=sem, vmem_limit_bytes=VMEM_LIMIT_BYTES)


def _sigmoid(x):
    return 1.0 / (1.0 + jnp.exp(-x))


def _softplus(x):
    return jnp.maximum(x, 0.0) + jnp.log1p(jnp.exp(-jnp.abs(x)))


def _dot(a, b):
    return jnp.dot(a, b, preferred_element_type=F32)


def _dot_nt(a, b):
    return lax.dot_general(a, b, (((1,), (1,)), ((), ())), preferred_element_type=F32)


def _dot_tn(a, b):
    return lax.dot_general(a, b, (((0,), (0,)), ((), ())), preferred_element_type=F32)


def _tile(n, pref):
    t = min(n, pref)
    assert n % t == 0, (n, pref)
    return t


def _ada_kernel(c_ref, w_ref, b_ref, o_ref):
    c = c_ref[...]
    ca = c * _sigmoid(c)
    o_ref[...] = jnp.sum(w_ref[...] * ca, axis=0, keepdims=True) + b_ref[...]


def _ada_mod(c_col, ada_w, ada_b):
    d, n = ada_w.shape
    tn = _tile(n, 1024)
    return pl.pallas_call(
        _ada_kernel,
        grid=(n // tn,),
        in_specs=[pl.BlockSpec((d, 1), lambda j: (0, 0)),
                  pl.BlockSpec((d, tn), lambda j: (0, j)),
                  pl.BlockSpec((1, tn), lambda j: (0, j))],
        out_specs=pl.BlockSpec((1, tn), lambda j: (0, j)),
        out_shape=jax.ShapeDtypeStruct((1, n), F32),
        compiler_params=_cparams("parallel"),
        name="ada_mod",
    )(c_col, ada_w, ada_b)


def _split3(v):
    hi = v.astype(BF16)
    r1 = v - hi.astype(F32)
    mid = r1.astype(BF16)
    lo = (r1 - mid.astype(F32)).astype(BF16)
    return hi, mid, lo


def _prenorm_kernel(x_ref, nw_ref, shift_ref, scale_ref, wg_ref, bias_ref, alog_ref, h_ref, g_ref, *,
                    a_heads, b_heads):
    x = x_ref[...]
    y = x * lax.rsqrt(jnp.mean(x * x, axis=-1, keepdims=True) + EPS) * nw_ref[...]
    hb = (y * (1.0 + scale_ref[...]) + shift_ref[...]).astype(BF16)
    h_ref[...] = hb

    tm = x.shape[0]
    z = _dot(hb, wg_ref[...]) + bias_ref[...]
    lane = lax.broadcasted_iota(jnp.int32, z.shape, 1)
    na, nb = 2 * a_heads, 2 * b_heads
    sp = _softplus(z)
    val = jnp.where(lane < na, z,
          jnp.where(lane < 2 * na, z - sp,
          jnp.where(lane < 2 * na + nb, _sigmoid(z),
          jnp.where(lane < 2 * na + 2 * nb, -jnp.exp(alog_ref[...]) * sp, 0.0))))

    r = lax.broadcasted_iota(jnp.int32, (tm, tm), 0)
    c = lax.broadcasted_iota(jnp.int32, (tm, tm), 1)
    parts = _split3(val)

    def cums(chunk, lower):
        same = (r // chunk) == (c // chunk)
        m = jnp.where(same & ((c <= r) if lower else (c >= r)), 1.0, 0.0).astype(BF16)
        return _dot(m, parts[0]) + _dot(m, parts[1]) + _dot(m, parts[2])

    la = min(MLSTM_CHUNK, tm)
    lb = min(GDN_CHUNK, tm)
    o_f, o_g = na, 2 * na + nb
    out = val
    out = jnp.where((lane >= o_f) & (lane < o_f + a_heads), cums(la, True), out)
    out = jnp.where((lane >= o_f + a_heads) & (lane < o_f + na), cums(la, False), out)
    out = jnp.where((lane >= o_g) & (lane < o_g + b_heads), cums(lb, True), out)
    out = jnp.where((lane >= o_g + b_heads) & (lane < o_g + nb), cums(lb, False), out)
    g_ref[...] = out


def _prenorm_gates(x2d, norm_w, mod, wg, bias_row, alog_row, a_heads, b_heads):
    t, d = x2d.shape
    tm = _tile(t, 512)
    row = lambda j: pl.BlockSpec((1, d), lambda i: (0, j))
    full = pl.BlockSpec((1, GATE_COLS), lambda i: (0, 0))
    return pl.pallas_call(
        functools.partial(_prenorm_kernel, a_heads=a_heads, b_heads=b_heads),
        grid=(t // tm,),
        in_specs=[pl.BlockSpec((tm, d), lambda i: (i, 0)),
                  pl.BlockSpec((1, d), lambda i: (0, 0)),
                  row(0), row(1),
                  pl.BlockSpec((d, GATE_COLS), lambda i: (0, 0)),
                  full, full],
        out_specs=[pl.BlockSpec((tm, d), lambda i: (i, 0)),
                   pl.BlockSpec((tm, GATE_COLS), lambda i: (i, 0))],
        out_shape=[jax.ShapeDtypeStruct((t, d), BF16),
                   jax.ShapeDtypeStruct((t, GATE_COLS), F32)],
        compiler_params=_cparams("parallel"),
        name="prenorm_gates",
    )(x2d, norm_w, mod, mod, wg, bias_row, alog_row)


def _mm_kernel(a_ref, b_ref, o_ref):
    o_ref[...] = _dot(a_ref[...], b_ref[...]).astype(o_ref.dtype)


def _matmul(a, b, out_dtype, tm_pref=1024, tn_pref=1024):
    m, k = a.shape
    _, n = b.shape
    tm, tn = _tile(m, tm_pref), _tile(n, tn_pref)
    return pl.pallas_call(
        _mm_kernel,
        grid=(n // tn, m // tm),
        in_specs=[pl.BlockSpec((tm, k), lambda j, i: (i, 0)),
                  pl.BlockSpec((k, tn), lambda j, i: (0, j))],
        out_specs=pl.BlockSpec((tm, tn), lambda j, i: (i, j)),
        out_shape=jax.ShapeDtypeStruct((m, n), out_dtype),
        compiler_params=_cparams("parallel", "parallel"),
        name="in_proj",
    )(a, b)


def _mlstm_kernel(qf_ref, kf_ref, vf_ref, qb_ref, kb_ref, vb_ref, gcf_ref, gcb_ref, grf_ref, grb_ref,
                  hf_ref, hb_ref, c_scr, n_scr, m_scr, *, heads, dk, dv):
    @pl.when(pl.program_id(0) == 0)
    def _():
        c_scr[...] = jnp.zeros_like(c_scr)
        n_scr[...] = jnp.zeros_like(n_scr)
        m_scr[...] = jnp.zeros_like(m_scr)

    L = qf_ref.shape[0]
    row = lax.broadcasted_iota(jnp.int32, (L, L), 0)
    col = lax.broadcasted_iota(jnp.int32, (L, L), 1)
    scale = dk ** -0.5
    dirs = ((qf_ref, kf_ref, vf_ref, gcf_ref, grf_ref, hf_ref),
            (qb_ref, kb_ref, vb_ref, gcb_ref, grb_ref, hb_ref))
    for d, (q_ref, k_ref, v_ref, gc_ref, gr_ref, out_ref) in enumerate(dirs):
        mask = (col <= row) if d == 0 else (col >= row)
        gc = gc_ref[...]
        gr = gr_ref[...]
        for h in range(heads):
            j = d * heads + h
            q = (q_ref[:, h * dk:(h + 1) * dk].astype(F32) * scale).astype(BF16)
            k = k_ref[:, h * dk:(h + 1) * dk]
            v = v_ref[:, h * dv:(h + 1) * dv]
            li_c = gc[:, j:j + 1]
            fc_c = gc[:, 2 * heads + j:2 * heads + j + 1]
            b_r = gr[j:j + 1, :] - gr[2 * heads + j:2 * heads + j + 1, :]
            m_mem = m_scr[j][:, 0:1]
            c_mem = c_scr[j]
            n_mem = n_scr[j]

            s = _dot_nt(q, k)
            d_log = jnp.where(mask, fc_c + b_r, -jnp.inf)
            inter = fc_c + m_mem
            m_t = jnp.maximum(inter, jnp.max(d_log, axis=-1, keepdims=True))
            p = jnp.exp(d_log - m_t) * s
            w_inter = jnp.exp(inter - m_t)
            num = w_inter * _dot(q, c_mem.astype(BF16)) + _dot(p.astype(BF16), v)
            qn = jnp.sum(q.astype(F32) * n_mem, axis=-1, keepdims=True)
            den = w_inter * qn + jnp.sum(p, axis=-1, keepdims=True)
            inv = 1.0 / jnp.maximum(jnp.abs(den), jnp.exp(-m_t))
            out_ref[:, h * dv:(h + 1) * dv] = (num * inv).astype(out_ref.dtype)

            f_tot = fc_c[L - 1:L, :] if d == 0 else fc_c[0:1, :]
            a_log = f_tot - fc_c + li_c
            m_new = jnp.maximum(f_tot + m_mem, jnp.max(a_log, axis=0, keepdims=True))
            w_s = jnp.exp(a_log - m_new)
            carry = jnp.exp(f_tot + m_mem - m_new)
            vw = (v.astype(F32) * w_s).astype(BF16)
            c_scr[j] = carry * c_mem + _dot_tn(k, vw)
            n_scr[j] = carry * n_mem + jnp.sum(k.astype(F32) * w_s, axis=0, keepdims=True)
            m_scr[j] = jnp.broadcast_to(m_new, (1, LANES))


def _mlstm(proj, gcol, grow, heads, dk, dv, q_off, k_off, v_off):
    t = proj.shape[0]
    L = _tile(t, MLSTM_CHUNK)
    nc = t // L
    qk_w, v_w = heads * dk, heads * dv
    fwd = lambda c: c
    bwd = lambda c: nc - 1 - c

    def spec(width, off, cm):
        assert off % width == 0
        return pl.BlockSpec((L, width), lambda c: (cm(c), off // width))

    in_specs = []
    for cm in (fwd, bwd):
        in_specs += [spec(qk_w, q_off, cm), spec(qk_w, k_off, cm), spec(v_w, v_off, cm)]
    in_specs += [pl.BlockSpec((L, GATE_COLS), lambda c: (c, 0)),
                 pl.BlockSpec((L, GATE_COLS), lambda c: (nc - 1 - c, 0)),
                 pl.BlockSpec((grow.shape[0], L), lambda c: (0, c)),
                 pl.BlockSpec((grow.shape[0], L), lambda c: (0, nc - 1 - c))]
    out_specs = [pl.BlockSpec((L, v_w), lambda c: (c, 0)),
                 pl.BlockSpec((L, v_w), lambda c: (nc - 1 - c, 0))]
    return pl.pallas_call(
        functools.partial(_mlstm_kernel, heads=heads, dk=dk, dv=dv),
        grid=(nc,),
        in_specs=in_specs,
        out_specs=out_specs,
        out_shape=[jax.ShapeDtypeStruct((t, v_w), BF16)] * 2,
        scratch_shapes=[pltpu.VMEM((2 * heads, dk, dv), F32),
                        pltpu.VMEM((2 * heads, 1, dk), F32),
                        pltpu.VMEM((2 * heads, 1, LANES), F32)],
        compiler_params=_cparams("arbitrary"),
        name="mlstm",
    )(proj, proj, proj, proj, proj, proj, gcol, gcol, grow, grow)


def _conv_kernel(x_ref, w_ref, o_ref, pad_scr, *, n_norm, n_scaled, qscale, ksize, rows):
    t = x_ref.shape[0]
    halo = 8
    nblk = t // rows
    zeros = jnp.zeros((halo, LANES), F32)
    pad_scr[0:halo, :] = zeros
    pad_scr[t + halo:t + 2 * halo, :] = zeros

    def fill(i, carry):
        r0 = pl.multiple_of(i * rows, rows)
        pad_scr[pl.ds(r0 + halo, rows), :] = x_ref[pl.ds(r0, rows), :].astype(F32)
        return carry

    lax.fori_loop(0, nblk, fill, 0)

    w = w_ref[...]
    j = pl.program_id(0)
    do_norm = j < n_norm
    post = jnp.where(j < n_scaled, qscale, 1.0)
    base = halo - ksize // 2

    def conv(i, carry):
        r0 = pl.multiple_of(i * rows, rows)
        win = pad_scr[pl.ds(r0, rows + 2 * halo), :]
        y = win[base:base + rows, :] * w[0:1, :]
        for kk in range(1, ksize):
            y = y + win[base + kk:base + kk + rows, :] * w[kk:kk + 1, :]
        y = y * _sigmoid(y)
        yn = y * lax.rsqrt(jnp.sum(y * y, axis=-1, keepdims=True) + EPS) * post
        o_ref[pl.ds(r0, rows), :] = jnp.where(do_norm, yn, y).astype(o_ref.dtype)
        return carry

    lax.fori_loop(0, nblk, conv, 0)


def _gdn_conv(proj, conv_w, col_off, heads, dk, dv):
    assert dk == LANES and dv == LANES
    t = proj.shape[0]
    ksize, width = conv_w.shape
    n_tiles = width // LANES
    assert col_off % LANES == 0
    rows = _tile(t, 256)
    return pl.pallas_call(
        functools.partial(_conv_kernel, n_norm=2 * heads, n_scaled=heads, qscale=dk ** -0.5, ksize=ksize,
                          rows=rows),
        grid=(n_tiles,),
        in_specs=[pl.BlockSpec((t, LANES), lambda j: (0, col_off // LANES + j)),
                  pl.BlockSpec((ksize, LANES), lambda j: (0, j))],
        out_specs=pl.BlockSpec((None, t, LANES), lambda j: (j, 0, 0)),
        out_shape=jax.ShapeDtypeStruct((n_tiles, t, LANES), BF16),
        scratch_shapes=[pltpu.VMEM((t + 16, LANES), F32)],
        compiler_params=_cparams("parallel"),
        name="gdn_conv",
    )(proj, conv_w)


def _unit_tri_inverse(a, row, col):
    L = a.shape[0]
    mm = lambda x, y: _dot(x.astype(BF16), y.astype(BF16))
    leaf = min(16, L)
    same = lambda n: (row // n) == (col // n)
    n_ = jnp.where(same(leaf), -a, 0.0)
    x = jnp.where(row == col, 1.0, n_)
    p = n_
    steps = 1
    while steps * 2 < leaf:
        p = mm(p, p)
        x = x + mm(p, x)
        steps *= 2
    n = leaf
    while n < L:
        off = jnp.where(same(2 * n) & jnp.logical_not(same(n)), a, 0.0)
        x = x - mm(x, mm(off, x))
        n *= 2
    return x


def _gdn_kernel(qf_ref, kf_ref, vf_ref, qb_ref, kb_ref, vb_ref, gcf_ref, gcb_ref, grf_ref, grb_ref,
                of_ref, ob_ref, s_scr, *, heads, unroll):
    @pl.when(pl.program_id(0) == 0)
    def _():
        s_scr[...] = jnp.zeros_like(s_scr)

    L = qf_ref.shape[1]
    dk = qf_ref.shape[2]
    row = lax.broadcasted_iota(jnp.int32, (L, L), 0)
    col = lax.broadcasted_iota(jnp.int32, (L, L), 1)
    dirs = ((qf_ref, kf_ref, vf_ref, gcf_ref, grf_ref, of_ref),
            (qb_ref, kb_ref, vb_ref, gcb_ref, grb_ref, ob_ref))

    def body(h, carry):
        for d, (q_ref, k_ref, v_ref, gc_ref, gr_ref, out_ref) in enumerate(dirs):
            incl = (col <= row) if d == 0 else (col >= row)
            strict = (col < row) if d == 0 else (col > row)
            q = q_ref[h]
            k = k_ref[h]
            v = v_ref[h]
            gcol = gc_ref[h]
            beta = gcol[:, 2 * d:2 * d + 1]
            g = gcol[:, 2 * d + 1:2 * d + 2]
            g_r = gr_ref[h, 0][d:d + 1, :]

            dec = jnp.exp(jnp.where(incl, g - g_r, -jnp.inf))
            kk = _dot_nt(k, k)
            qk = _dot_nt(q, k)
            a = jnp.where(strict, beta * kk * dec, 0.0)
            t_mat = _unit_tri_inverse(a, row, col)

            kf = k.astype(F32)
            eg = jnp.exp(g)
            rhs = jnp.concatenate([beta * v.astype(F32), (beta * eg) * kf], axis=1).astype(BF16)
            uw = _dot(t_mat.astype(BF16), rhs)
            dv = v.shape[1]
            u, w = uw[:, :dv], uw[:, dv:]
            s_mem = s_scr[d, h]
            s_b = s_mem.astype(BF16)
            v_new = u - _dot(w.astype(BF16), s_b)
            v_new_b = v_new.astype(BF16)
            attn = (qk * dec).astype(BF16)
            qg = (q.astype(F32) * eg).astype(BF16)
            out_ref[h] = (_dot(qg, s_b) + _dot(attn, v_new_b)).astype(out_ref.dtype)
            g_last = g[L - 1:L, :] if d == 0 else g[0:1, :]
            kg = (kf * jnp.exp(g_last - g)).astype(BF16)
            s_scr[d, h] = jnp.exp(g_last) * s_mem + _dot_tn(kg, v_new_b)
        return carry

    lax.fori_loop(0, heads, body, 0, unroll=unroll)


def _gdn(qkv, gb_col, gb_row, heads):
    _, t, dk = qkv.shape
    L = _tile(t, GDN_CHUNK)
    nc = t // L
    fwd = lambda c: c
    bwd = lambda c: nc - 1 - c
    in_specs = []
    for cm in (fwd, bwd):
        for part in range(3):
            in_specs.append(pl.BlockSpec((heads, L, dk), lambda c, cm=cm, part=part: (part, cm(c), 0)))
    for cm in (fwd, bwd):
        in_specs.append(pl.BlockSpec((heads, L, 4), lambda c, cm=cm: (0, cm(c), 0)))
    for cm in (fwd, bwd):
        in_specs.append(pl.BlockSpec((heads, 1, 2, L), lambda c, cm=cm: (0, cm(c), 0, 0)))
    out_specs = [pl.BlockSpec((heads, L, dk), lambda c: (0, c, 0)),
                 pl.BlockSpec((heads, L, dk), lambda c: (0, nc - 1 - c, 0))]
    return pl.pallas_call(
        functools.partial(_gdn_kernel, heads=heads, unroll=2),
        grid=(nc,),
        in_specs=in_specs,
        out_specs=out_specs,
        out_shape=[jax.ShapeDtypeStruct((heads, t, dk), BF16)] * 2,
        scratch_shapes=[pltpu.VMEM((2, heads, dk, dk), F32)],
        compiler_params=_cparams("arbitrary"),
        name="gdn",
    )(qkv, qkv, qkv, qkv, qkv, qkv, gb_col, gb_col, gb_row, gb_row)


def _merge_kernel(hf_ref, hb_ref, ao_ref, anw_ref, of_ref, ob_ref, z_ref, bnw_ref, ga_ref, gb_ref,
                  wa_ref, wb_ref, o_ref, *, a_heads, a_dv, b_heads, b_dv):
    segs = []
    for h in range(a_heads):
        sl = slice(h * a_dv, (h + 1) * a_dv)
        s = hf_ref[:, sl].astype(F32) + hb_ref[:, sl].astype(F32)
        s = s * lax.rsqrt(jnp.mean(s * s, axis=-1, keepdims=True) + EPS) * anw_ref[:, sl]
        segs.append((_sigmoid(ao_ref[:, sl].astype(F32)) * s).astype(BF16))
    y_a = _dot(jnp.concatenate(segs, axis=1), wa_ref[...])
    segs = []
    for h in range(b_heads):
        sl = slice(h * b_dv, (h + 1) * b_dv)
        s = of_ref[h].astype(F32) + ob_ref[h].astype(F32)
        s = s * lax.rsqrt(jnp.mean(s * s, axis=-1, keepdims=True) + EPS) * bnw_ref[:, sl]
        z = z_ref[:, sl].astype(F32)
        segs.append((s * (z * _sigmoid(z))).astype(BF16))
    y_b = _dot(jnp.concatenate(segs, axis=1), wb_ref[...])
    mix = _sigmoid(ga_ref[...].astype(F32)) * y_a + _sigmoid(gb_ref[...].astype(F32)) * y_b
    o_ref[...] = mix.astype(o_ref.dtype)


def _branch_merge(hf, hb, proj, a_norm_w, o_f, o_b, b_norm_w, w_a, w_b, ao_off, z_off, merge_off,
                  a_heads, a_dv, b_heads, b_dv):
    t, av = hf.shape
    bv = b_heads * b_dv
    d = w_a.shape[1]
    assert av == bv == d
    tm = _tile(t, 256)
    pcol = lambda off: pl.BlockSpec((tm, d), lambda i: (i, off // d))
    rowv = pl.BlockSpec((1, d), lambda i: (0, 0))
    wspec = pl.BlockSpec((d, d), lambda i: (0, 0), pipeline_mode=pl.Buffered(1))
    hm = pl.BlockSpec((b_heads, tm, b_dv), lambda i: (0, i, 0))
    for off in (ao_off, z_off, merge_off):
        assert off % d == 0
    return pl.pallas_call(
        functools.partial(_merge_kernel, a_heads=a_heads, a_dv=a_dv, b_heads=b_heads, b_dv=b_dv),
        grid=(t // tm,),
        in_specs=[pl.BlockSpec((tm, d), lambda i: (i, 0)), pl.BlockSpec((tm, d), lambda i: (i, 0)),
                  pcol(ao_off), rowv, hm, hm, pcol(z_off), rowv, pcol(merge_off), pcol(merge_off + d),
                  wspec, wspec],
        out_specs=pl.BlockSpec((tm, d), lambda i: (i, 0)),
        out_shape=jax.ShapeDtypeStruct((t, d), BF16),
        compiler_params=_cparams("parallel"),
        name="branch_merge",
    )(hf, hb, proj, a_norm_w, o_f, o_b, proj, b_norm_w, proj, proj, w_a, w_b)


def _outproj_kernel(mix_ref, w_ref, x_ref, gate_ref, nw_ref, shift_ref, scale_ref, x1_ref, hf_ref):
    x1 = x_ref[...] + gate_ref[...] * _dot(mix_ref[...], w_ref[...])
    x1_ref[...] = x1
    y = x1 * lax.rsqrt(jnp.mean(x1 * x1, axis=-1, keepdims=True) + EPS) * nw_ref[...]
    hf_ref[...] = (y * (1.0 + scale_ref[...]) + shift_ref[...]).astype(hf_ref.dtype)


def _out_proj(mix, w_out, x2d, mod, norm_w):
    t, d = x2d.shape
    tm = _tile(t, 512)
    row = lambda j: pl.BlockSpec((1, d), lambda i: (0, j))
    tile = pl.BlockSpec((tm, d), lambda i: (i, 0))
    return pl.pallas_call(
        _outproj_kernel,
        grid=(t // tm,),
        in_specs=[tile, pl.BlockSpec((d, d), lambda i: (0, 0), pipeline_mode=pl.Buffered(1)), tile,
                  row(2), pl.BlockSpec((1, d), lambda i: (0, 0)), row(3), row(4)],
        out_specs=[tile, tile],
        out_shape=[jax.ShapeDtypeStruct((t, d), F32), jax.ShapeDtypeStruct((t, d), BF16)],
        compiler_params=_cparams("parallel"),
        name="out_proj",
    )(mix, w_out, x2d, mod, norm_w, mod, mod)


def _ffn_up_kernel(h_ref, wg_ref, wu_ref, o_ref):
    h = h_ref[...]
    g = _dot(h, wg_ref[...])
    u = _dot(h, wu_ref[...])
    o_ref[...] = (g * _sigmoid(g) * u).astype(o_ref.dtype)


def _ffn_up(hf, w_gate_up):
    t, d = hf.shape
    dff = w_gate_up.shape[1] // 2
    tm, tn = _tile(t, 1024), _tile(dff, 512)
    nj = dff // tn
    return pl.pallas_call(
        _ffn_up_kernel,
        grid=(nj, t // tm),
        in_specs=[pl.BlockSpec((tm, d), lambda j, i: (i, 0)),
                  pl.BlockSpec((d, tn), lambda j, i: (0, j)),
                  pl.BlockSpec((d, tn), lambda j, i: (0, nj + j))],
        out_specs=pl.BlockSpec((tm, tn), lambda j, i: (i, j)),
        out_shape=jax.ShapeDtypeStruct((t, dff), BF16),
        compiler_params=_cparams("parallel", "parallel"),
        name="ffn_up",
    )(hf, w_gate_up, w_gate_up)


def _ffn_down_kernel(a_ref, w_ref, x_ref, gate_ref, nw_ref, o_ref, acc_ref, *, final_norm):
    kstep = pl.program_id(1)

    @pl.when(kstep == 0)
    def _():
        acc_ref[...] = jnp.zeros_like(acc_ref)

    acc_ref[...] += _dot(a_ref[...], w_ref[...])

    @pl.when(kstep == pl.num_programs(1) - 1)
    def _():
        x2 = x_ref[...] + gate_ref[...] * acc_ref[...]
        if final_norm:
            x2 = x2 * lax.rsqrt(jnp.mean(x2 * x2, axis=-1, keepdims=True) + EPS) * nw_ref[...]
        o_ref[...] = x2


def _ffn_down(act, w_down, x1, mod, norm_w, final_norm):
    t, dff = act.shape
    d = w_down.shape[1]
    tm = _tile(t, 512)
    tk = dff
    for cand in (1408, 1024, 512, 256, 128):
        if dff % cand == 0:
            tk = cand
            break
    return pl.pallas_call(
        functools.partial(_ffn_down_kernel, final_norm=final_norm),
        grid=(t // tm, dff // tk),
        in_specs=[pl.BlockSpec((tm, tk), lambda i, k: (i, k)),
                  pl.BlockSpec((tk, d), lambda i, k: (k, 0)),
                  pl.BlockSpec((tm, d), lambda i, k: (i, 0)),
                  pl.BlockSpec((1, d), lambda i, k: (0, 5)),
                  pl.BlockSpec((1, d), lambda i, k: (0, 0))],
        out_specs=pl.BlockSpec((tm, d), lambda i, k: (i, 0)),
        out_shape=jax.ShapeDtypeStruct((t, d), F32),
        scratch_shapes=[pltpu.VMEM((tm, d), F32)],
        compiler_params=_cparams("parallel", "arbitrary"),
        name="ffn_down",
    )(act, w_down, x1, mod, norm_w)


def _layer(x2d, c_col, ada_w, ada_b, norm_mix_w, w_in, mlstm_b_i, mlstm_b_f, mlstm_norm_w, gdn_conv_w,
           gdn_A_log, gdn_dt_bias, gdn_norm_w, w_branch_a, w_branch_b, w_out, norm_ffn_w, w_gate_up, w_down,
           final_norm_w, final_norm):
    t, d = x2d.shape
    a_heads = mlstm_b_i.shape[-1]
    b_heads = gdn_A_log.shape[-1]
    a_v = mlstm_norm_w.shape[-1]
    b_v = gdn_norm_w.shape[-1]
    a_dv, b_dv = a_v // a_heads, b_v // b_heads
    b_qkv = gdn_conv_w.shape[-1]
    b_dk = (b_qkv - b_v) // (2 * b_heads)
    d_in = w_in.shape[-1]
    na, nb = 2 * a_heads, 2 * b_heads
    a_qk = (d_in - 2 * a_v - 2 * na - b_qkv - b_v - 2 * nb - 2 * d) // 2
    a_dk = a_qk // a_heads
    assert 2 * na + 2 * nb <= GATE_COLS

    sizes = (a_qk, a_qk, a_v, a_v, na, na, b_qkv, b_v, nb, nb, 2 * d)
    offs = [0]
    for s in sizes:
        offs.append(offs[-1] + s)
    assert offs[-1] == d_in
    w_main = jnp.concatenate([w_in[:, offs[0]:offs[4]], w_in[:, offs[6]:offs[8]], w_in[:, offs[10]:offs[11]]],
                             axis=1).astype(BF16)
    w_gate = jnp.concatenate([w_in[:, offs[4]:offs[6]], w_in[:, offs[8]:offs[10]],
                              jnp.zeros((d, GATE_COLS - 2 * na - 2 * nb), F32)], axis=1).astype(BF16)
    q_off, k_off, v_off, ao_off = 0, a_qk, 2 * a_qk, 2 * a_qk + a_v
    bqkv_off = ao_off + a_v
    z_off = bqkv_off + b_qkv
    merge_off = z_off + b_v

    pad = jnp.zeros((GATE_COLS - 2 * na - 2 * nb,), F32)
    bias_row = jnp.concatenate([mlstm_b_i.reshape(-1), mlstm_b_f.reshape(-1), jnp.zeros((nb,), F32),
                                gdn_dt_bias.reshape(-1), pad]).reshape(1, GATE_COLS)
    alog_row = jnp.concatenate([jnp.zeros((2 * na + nb,), F32), gdn_A_log.reshape(-1), pad]).reshape(1, GATE_COLS)

    mod = _ada_mod(c_col, ada_w, ada_b.reshape(1, -1))
    h, gcol = _prenorm_gates(x2d, norm_mix_w.reshape(1, d), mod, w_gate, bias_row, alog_row, a_heads, b_heads)
    proj = _matmul(h, w_main, BF16)

    grow_a = gcol[:, :2 * na].T
    h_f, h_b = _mlstm(proj, gcol, grow_a, a_heads, a_dk, a_dv, q_off, k_off, v_off)

    lb = min(GDN_CHUNK, t)
    g_beta = gcol[:, 2 * na:2 * na + nb].reshape(t, 2, b_heads)
    g_dec = gcol[:, 2 * na + nb:2 * na + 2 * nb].reshape(t, 2, b_heads)
    gb_col = jnp.stack([g_beta[:, 0], g_dec[:, 0], g_beta[:, 1], g_dec[:, 1]], axis=-1).transpose(1, 0, 2)
    gb_row = g_dec.reshape(t // lb, lb, 2, b_heads).transpose(3, 0, 2, 1)
    qkv = _gdn_conv(proj, gdn_conv_w, bqkv_off, b_heads, b_dk, b_dv)
    o_f, o_b = _gdn(qkv, gb_col, gb_row, b_heads)

    mix = _branch_merge(h_f, h_b, proj, mlstm_norm_w.reshape(1, a_v), o_f, o_b, gdn_norm_w.reshape(1, b_v),
                        w_branch_a.astype(BF16), w_branch_b.astype(BF16), ao_off, z_off, merge_off,
                        a_heads, a_dv, b_heads, b_dv)
    x1, hf = _out_proj(mix, w_out.astype(BF16), x2d, mod, norm_ffn_w.reshape(1, d))
    act = _ffn_up(hf, w_gate_up.astype(BF16))
    return _ffn_down(act, w_down.astype(BF16), x1, mod, final_norm_w.reshape(1, d), final_norm)


def kernel(x, c, ada_w, ada_b, norm_mix_w, w_in, mlstm_b_i, mlstm_b_f, mlstm_norm_w, gdn_conv_w, gdn_A_log,
           gdn_dt_bias, gdn_norm_w, w_branch_a, w_branch_b, w_out, norm_ffn_w, w_gate_up, w_down, norm_final_w):
    bsz, t, d = x.shape
    depth = ada_w.shape[0]
    outs = []
    for b in range(bsz):
        xb = x[b]
        c_col = c[b].reshape(d, 1)
        for l in range(depth):
            xb = _layer(xb, c_col, ada_w[l], ada_b[l], norm_mix_w[l], w_in[l], mlstm_b_i[l], mlstm_b_f[l],
                        mlstm_norm_w[l], gdn_conv_w[l], gdn_A_log[l], gdn_dt_bias[l], gdn_norm_w[l],
                        w_branch_a[l], w_branch_b[l], w_out[l], norm_ffn_w[l], w_gate_up[l], w_down[l],
                        norm_final_w, l == depth - 1)
        outs.append(xb)
    return jnp.stack(outs, axis=0)
```
